```python
import jax, jax.numpy as jnp
from jax import lax
import numpy as np

D_MODEL = 1024
BATCH = 32
SEQ = 2048
DEPTH = 2

HEAD_DIM = 64
A_Q_HEADS = 8
A_KV_HEADS = 2
A_WINDOW = 128
B_HEADS = 8
B_PATTERNS = ((128, 1), (512, 4), (2048, 16))
ROPE_THETA = 10000.0
BLOCK = 128
C_HEADS = 16
C_Q_RANK = 384
C_KV_RANK = 256
C_NOPE_DIM = 64
C_ROPE_DIM = 32
C_V_DIM = 64
C_QK_DIM = C_NOPE_DIM + C_ROPE_DIM
D_FF = 2816
N_EXPERTS = 8
TOP_K = 2
D_FF_EXPERT = 3584
MOE_BLOCK = 256
EPS = 1e-6

A_Q_W = A_Q_HEADS * HEAD_DIM
A_KV_W = A_KV_HEADS * HEAD_DIM
B_W = B_HEADS * HEAD_DIM
L0_IN = A_Q_W + 2 * A_KV_W + 3 * B_W
L0_MIX = A_Q_W + B_W
L0_SPLITS = (A_Q_W, A_Q_W + A_KV_W, A_Q_W + 2 * A_KV_W, A_Q_W + 2 * A_KV_W + B_W, A_Q_W + 2 * A_KV_W + 2 * B_W)
L1_IN = C_Q_RANK + C_KV_RANK + C_ROPE_DIM

kernel_name = "hybrid_swa_dilated_mla_moe"


def rms_norm(x, g):
    xf = x.astype(jnp.float32)
    y = xf * lax.rsqrt(jnp.mean(xf * xf, axis=-1, keepdims=True) + EPS)
    return (y * g.astype(jnp.float32)).astype(x.dtype)


def rope_tables(positions, dim):
    inv = ROPE_THETA ** (-jnp.arange(0, dim, 2, dtype=jnp.float32) / dim)
    ang = positions.astype(jnp.float32)[..., None] * inv
    return jnp.cos(ang)[:, :, None, :], jnp.sin(ang)[:, :, None, :]


def apply_rope(t, cos, sin):
    half = t.shape[-1] // 2
    tf = t.astype(jnp.float32)
    t1, t2 = tf[..., :half], tf[..., half:]
    return jnp.concatenate([t1 * cos - t2 * sin, t2 * cos + t1 * sin], axis=-1).astype(t.dtype)


def swiglu(h, w_gate, w_up, w_down):
    return (jax.nn.silu(h @ w_gate) * (h @ w_up)) @ w_down


def banded_attention(q, k, v, max_dist, sinks):
    nb_, L, hq, dh = q.shape
    hkv = k.shape[2]
    g = hq // hkv
    nblk = -(-L // BLOCK)
    lp = nblk * BLOCK
    pad = ((0, 0), (0, lp - L), (0, 0), (0, 0))
    q, k, v = jnp.pad(q, pad), jnp.pad(k, pad), jnp.pad(v, pad)

    def two_blocks(t):
        prev = jnp.pad(t, ((0, 0), (BLOCK, 0), (0, 0), (0, 0)))[:, :lp]
        return jnp.concatenate([prev.reshape(nb_, nblk, BLOCK, hkv, dh),
                                t.reshape(nb_, nblk, BLOCK, hkv, dh)], axis=2)

    kb, vb = two_blocks(k), two_blocks(v)
    qb = q.reshape(nb_, nblk, BLOCK, hkv, g, dh)
    s = jnp.einsum('bnqhgd,bnkhd->bnhgqk', qb, kb, preferred_element_type=jnp.float32) * (dh ** -0.5)
    qi = jnp.arange(BLOCK)[:, None]
    kj = jnp.arange(2 * BLOCK)[None, :]
    dist = BLOCK + qi - kj
    kpos = jnp.arange(nblk)[:, None, None] * BLOCK - BLOCK + kj
    valid = (dist >= 0) & (dist <= max_dist) & (kpos >= 0)
    s = jnp.where(valid[None, :, None, None], s, -jnp.inf)
    m = jnp.max(s, axis=-1)
    if sinks is not None:
        sink = sinks.astype(jnp.float32).reshape(1, 1, hkv, g, 1)
        m = jnp.maximum(m, sink)
    p = jnp.exp(s - m[..., None])
    denom = jnp.sum(p, axis=-1)
    if sinks is not None:
        denom = denom + jnp.exp(sink - m)
    o = jnp.einsum('bnhgqk,bnkhd->bnhgqd', p.astype(v.dtype), vb,
                   preferred_element_type=jnp.float32) / denom[..., None]
    o = o.transpose(0, 1, 4, 2, 3, 5).reshape(nb_, lp, hq, dh)[:, :L]
    lse = (m + jnp.log(denom)).transpose(0, 1, 4, 2, 3).reshape(nb_, lp, hq)[:, :L]
    return o, lse


def dilated_mixture(q, k, v):
    bn, s, h, dh = q.shape
    outs, lses = [], []
    for window, dil in B_PATTERNS:
        def fold(t):
            rest = t.shape[2:]
            return t.reshape(bn, s // dil, dil, *rest).swapaxes(1, 2).reshape(bn * dil, s // dil, *rest)

        def unfold(t):
            rest = t.shape[2:]
            return t.reshape(bn, dil, s // dil, *rest).swapaxes(1, 2).reshape(bn, s, *rest)

        o, lse = banded_attention(fold(q), fold(k), fold(v), window // dil, None)
        outs.append(unfold(o))
        lses.append(unfold(lse))
    w = jax.nn.softmax(jnp.stack(lses), axis=0)
    return jnp.sum(w[..., None] * jnp.stack(outs), axis=0)


def causal_block_attention(q, k, v):
    bn, s, h, dqk = q.shape
    dv = v.shape[-1]
    nq = s // BLOCK
    qb = q.reshape(bn, nq, BLOCK, h, dqk).swapaxes(0, 1)
    key_pos = jnp.arange(s)
    scale = dqk ** -0.5

    def one(args):
        qi, n = args
        sc = jnp.einsum('bqhd,bkhd->bhqk', qi, k, preferred_element_type=jnp.float32) * scale
        qpos = n * BLOCK + jnp.arange(BLOCK)
        sc = jnp.where(qpos[:, None] >= key_pos[None, :], sc, -jnp.inf)
        p = jax.nn.softmax(sc, axis=-1)
        return jnp.einsum('bhqk,bkhd->bqhd', p.astype(v.dtype), v,
                          preferred_element_type=jnp.float32).astype(v.dtype)

    o = lax.map(one, (qb, jnp.arange(nq)))
    return o.swapaxes(0, 1).reshape(bn, s, h * dv)


def moe_swiglu(h, router, we_gate, we_up, we_down):
    bn, s, d = h.shape
    n = bn * s
    xt = h.reshape(n, d)
    logits = jnp.einsum('nd,de->ne', xt, router, preferred_element_type=jnp.float32)
    top_vals, top_idx = lax.top_k(logits, TOP_K)
    gates = jax.nn.softmax(top_vals, axis=-1)
    e_flat = top_idx.reshape(-1).astype(jnp.int32)
    tok_flat = jnp.repeat(jnp.arange(n, dtype=jnp.int32), TOP_K)
    g_flat = gates.reshape(-1)
    order = jnp.argsort(e_flat)
    e_sorted = e_flat[order]
    counts = jnp.zeros((N_EXPERTS,), jnp.int32).at[e_flat].add(1)
    starts = jnp.cumsum(counts) - counts
    padded = (counts + MOE_BLOCK - 1) // MOE_BLOCK * MOE_BLOCK
    pend = jnp.cumsum(padded)
    pstarts = pend - padded
    rank = jnp.arange(n * TOP_K, dtype=jnp.int32) - starts[e_sorted]
    dest = pstarts[e_sorted] + rank
    nb = -(-(n * TOP_K) // MOE_BLOCK) + N_EXPERTS
    p_rows = nb * MOE_BLOCK
    row_tok = jnp.full((p_rows,), n, jnp.int32).at[dest].set(tok_flat[order])
    row_gate = jnp.zeros((p_rows,), jnp.float32).at[dest].set(g_flat[order])
    block_e = jnp.minimum(jnp.searchsorted(pend, jnp.arange(nb, dtype=jnp.int32) * MOE_BLOCK, side='right'),
                          N_EXPERTS - 1)
    x_pad = jnp.concatenate([xt, jnp.zeros((1, d), xt.dtype)], axis=0)
    xs = x_pad[row_tok].reshape(nb, MOE_BLOCK, d)

    def expert_block(args):
        xb, e = args
        return swiglu(xb, we_gate[e], we_up[e], we_down[e])

    ys = lax.map(expert_block, (xs, block_e)).reshape(p_rows, d)
    ys = ys * row_gate[:, None].astype(ys.dtype)
    out = jnp.zeros((n + 1, d), ys.dtype).at[row_tok].add(ys)[:n]
    return out.reshape(bn, s, d)


def ab_layer(x, cos, sin, attn_norm, w_in, a_qn, a_kn, a_sinks, b_qn, b_kn, w_out,
             ffn_norm, w_gate, w_up, w_down):
    bn, s, _ = x.shape
    h = rms_norm(x, attn_norm)
    proj = h @ w_in
    aq, ak, av, bq, bk, bv = jnp.split(proj, L0_SPLITS, axis=-1)

    def heads(t, nh):
        return t.reshape(bn, s, nh, HEAD_DIM)

    aq = apply_rope(rms_norm(heads(aq, A_Q_HEADS), a_qn), cos, sin)
    ak = apply_rope(rms_norm(heads(ak, A_KV_HEADS), a_kn), cos, sin)
    a_out, _ = banded_attention(aq, ak, heads(av, A_KV_HEADS), A_WINDOW - 1, a_sinks)
    bq = apply_rope(rms_norm(heads(bq, B_HEADS), b_qn), cos, sin)
    bk = apply_rope(rms_norm(heads(bk, B_HEADS), b_kn), cos, sin)
    b_out = dilated_mixture(bq, bk, heads(bv, B_HEADS))
    mix = jnp.concatenate([a_out.reshape(bn, s, A_Q_W), b_out.reshape(bn, s, B_W)], axis=-1).astype(x.dtype)
    x = x + mix @ w_out
    return x + swiglu(rms_norm(x, ffn_norm), w_gate, w_up, w_down)


def mla_moe_layer(x, cos, sin, attn_norm, w_in, q_a_norm, w_uq, kv_a_norm, w_ukv, c_qn, c_kn, w_out,
                  ffn_norm, router, we_gate, we_up, we_down):
    bn, s, _ = x.shape
    h = rms_norm(x, attn_norm)
    proj = h @ w_in
    cq, ckv, kpe = jnp.split(proj, (C_Q_RANK, C_Q_RANK + C_KV_RANK), axis=-1)
    q = (rms_norm(cq, q_a_norm) @ w_uq).reshape(bn, s, C_HEADS, C_QK_DIM)
    kv = (rms_norm(ckv, kv_a_norm) @ w_ukv).reshape(bn, s, C_HEADS, C_NOPE_DIM + C_V_DIM)
    k_nope, v = kv[..., :C_NOPE_DIM], kv[..., C_NOPE_DIM:]
    k = jnp.concatenate([k_nope, jnp.broadcast_to(kpe[:, :, None, :], (bn, s, C_HEADS, C_ROPE_DIM))], axis=-1)
    q = rms_norm(q, c_qn)
    k = rms_norm(k, c_kn)
    q = jnp.concatenate([q[..., :C_NOPE_DIM], apply_rope(q[..., C_NOPE_DIM:], cos, sin)], axis=-1)
    k = jnp.concatenate([k[..., :C_NOPE_DIM], apply_rope(k[..., C_NOPE_DIM:], cos, sin)], axis=-1)
    attn = causal_block_attention(q, k, v)
    x = x + attn @ w_out
    return x + moe_swiglu(rms_norm(x, ffn_norm), router, we_gate, we_up, we_down)


def setup_inputs(seed: int = 0) -> dict:
    key = jax.random.key(seed)
    ks = iter(jax.random.split(key, 40))

    def w(shape, fan_in):
        return jax.random.normal(next(ks), shape, jnp.float32) * (fan_in ** -0.5)

    def gain(nd):
        return 1.0 + 0.05 * jax.random.normal(next(ks), (nd,), jnp.float32)

    x = jax.random.normal(next(ks), (BATCH, SEQ, D_MODEL), jnp.float32)
    offsets = jax.random.randint(next(ks), (BATCH, 1), 0, 4096, dtype=jnp.int32)
    positions = (offsets + jnp.arange(SEQ, dtype=jnp.int32)[None, :]).astype(jnp.int32)
    return {
        'x': x,
        'positions': positions,
        'l0_attn_norm': gain(D_MODEL),
        'l0_w_in': w((D_MODEL, L0_IN), D_MODEL),
        'l0_a_q_norm': gain(HEAD_DIM),
        'l0_a_k_norm': gain(HEAD_DIM),
        'l0_a_sinks': 0.5 * jax.random.normal(next(ks), (A_Q_HEADS,), jnp.float32),
        'l0_b_q_norm': gain(HEAD_DIM),
        'l0_b_k_norm': gain(HEAD_DIM),
        'l0_w_out': w((L0_MIX, D_MODEL), L0_MIX),
        'l0_ffn_norm': gain(D_MODEL),
        'l0_w_gate': w((D_MODEL, D_FF), D_MODEL),
        'l0_w_up': w((D_MODEL, D_FF), D_MODEL),
        'l0_w_down': w((D_FF, D_MODEL), D_FF),
        'l1_attn_norm': gain(D_MODEL),
        'l1_w_in': w((D_MODEL, L1_IN), D_MODEL),
        'l1_q_a_norm': gain(C_Q_RANK),
        'l1_w_uq': w((C_Q_RANK, C_HEADS * C_QK_DIM), C_Q_RANK),
        'l1_kv_a_norm': gain(C_KV_RANK),
        'l1_w_ukv': w((C_KV_RANK, C_HEADS * (C_NOPE_DIM + C_V_DIM)), C_KV_RANK),
        'l1_c_q_norm': gain(C_QK_DIM),
        'l1_c_k_norm': gain(C_QK_DIM),
        'l1_w_out': w((C_HEADS * C_V_DIM, D_MODEL), C_HEADS * C_V_DIM),
        'l1_ffn_norm': gain(D_MODEL),
        'l1_router': w((D_MODEL, N_EXPERTS), D_MODEL),
        'l1_we_gate': w((N_EXPERTS, D_MODEL, D_FF_EXPERT), D_MODEL),
        'l1_we_up': w((N_EXPERTS, D_MODEL, D_FF_EXPERT), D_MODEL),
        'l1_we_down': w((N_EXPERTS, D_FF_EXPERT, D_MODEL), D_FF_EXPERT),
    }


def reference(x, positions, l0_attn_norm, l0_w_in, l0_a_q_norm, l0_a_k_norm, l0_a_sinks, l0_b_q_norm,
              l0_b_k_norm, l0_w_out, l0_ffn_norm, l0_w_gate, l0_w_up, l0_w_down, l1_attn_norm, l1_w_in,
              l1_q_a_norm, l1_w_uq, l1_kv_a_norm, l1_w_ukv, l1_c_q_norm, l1_c_k_norm, l1_w_out, l1_ffn_norm,
              l1_router, l1_we_gate, l1_we_up, l1_we_down):
    cos_h, sin_h = rope_tables(positions, HEAD_DIM)
    cos_c, sin_c = rope_tables(positions, C_ROPE_DIM)
    layer_params = (
        (l0_attn_norm, l0_w_in, l0_a_q_norm, l0_a_k_norm, l0_a_sinks, l0_b_q_norm, l0_b_k_norm, l0_w_out,
         l0_ffn_norm, l0_w_gate, l0_w_up, l0_w_down),
        (l1_attn_norm, l1_w_in, l1_q_a_norm, l1_w_uq, l1_kv_a_norm, l1_w_ukv, l1_c_q_norm, l1_c_k_norm,
         l1_w_out, l1_ffn_norm, l1_router, l1_we_gate, l1_we_up, l1_we_down),
    )
    for layer in range(DEPTH):
        if layer % 2 == 0:
            x = ab_layer(x, cos_h, sin_h, *layer_params[layer])
        else:
            x = mla_moe_layer(x, cos_c, sin_c, *layer_params[layer])
    return x
```

```python
import functools

import numpy as np
import jax
import jax.numpy as jnp
from jax import lax
from jax.experimental import pallas as pl
from jax.experimental.pallas import tpu as pltpu

F32 = jnp.float32
BF16 = jnp.bfloat16

D_MODEL = 1024
SEQ = 2048
HEAD_DIM = 64
A_Q_HEADS = 8
A_KV_HEADS = 2
A_WINDOW = 128
B_HEADS = 8
B_PATTERNS = ((128, 1), (512, 4), (2048, 16))
ROPE_THETA = 10000.0
BLOCK = 128
C_HEADS = 16
C_Q_RANK = 384
C_KV_RANK = 256
C_NOPE_DIM = 64
C_ROPE_DIM = 32
C_V_DIM = 64
C_QK_DIM = C_NOPE_DIM + C_ROPE_DIM
D_FF = 2816
N_EXPERTS = 8
TOP_K = 2
D_FF_EXPERT = 3584
EPS = 1e-6

LANES = 128
NEG = -1e30

A_Q_W = A_Q_HEADS * HEAD_DIM
A_KV_W = A_KV_HEADS * HEAD_DIM
B_W = B_HEADS * HEAD_DIM
L0_IN = A_Q_W + 2 * A_KV_W + 3 * B_W

TM_PROJ = 512
TM_FFN = 1024
TF_FFN = 256
TM_ROUTE = 512
TM_EXPERT = 1024
TF_EXPERT = 512
TM_COMBINE = 512
TQ_MLA = 256

VMEM_LIMIT = 56 * 1024 * 1024


def _cparams(sem):
    return pltpu.CompilerParams(dimension_semantics=sem, vmem_limit_bytes=VMEM_LIMIT)


def _nt_dot(a, b):
    return lax.dot_general(a, b, (((1,), (1,)), ((), ())), preferred_element_type=F32)


def _rms(x, g):
    ms = jnp.mean(x * x, axis=-1, keepdims=True)
    return x * lax.rsqrt(ms + EPS) * g


def _rope_tables_kernel(pos_ref, inv_ref, cos_ref, sin_ref):
    ang = pos_ref[...].astype(F32) * inv_ref[...]
    cos_ref[...] = jnp.cos(ang)
    sin_ref[...] = jnp.sin(ang)


def _rope_tables(positions, dim):
    nf = dim // 2
    t = positions.size
    inv = ROPE_THETA ** (-jnp.arange(0, dim, 2, dtype=F32) / dim)
    rows = t * nf // LANES
    pos_rep = jnp.repeat(positions.reshape(-1), nf).reshape(rows, LANES)
    inv_t = jnp.tile(inv, LANES // nf).reshape(1, LANES)
    br = min(rows, 2048)
    spec = pl.BlockSpec((br, LANES), lambda i: (i, 0))
    cos, sin = pl.pallas_call(
        _rope_tables_kernel,
        grid=(rows // br,),
        in_specs=[spec, pl.BlockSpec((1, LANES), lambda i: (0, 0))],
        out_specs=[spec, spec],
        out_shape=[jax.ShapeDtypeStruct((rows, LANES), F32)] * 2,
        compiler_params=_cparams(("arbitrary",)),
        name="rope_tables",
    )(pos_rep, inv_t)
    return cos.reshape(t, nf), sin.reshape(t, nf)


_L0_NORMED_BLOCKS = (0, 1, 2, 3, 4, 6, 7, 8, 9, 10, 11, 12, 13)
_L0_Q_BLOCKS = (0, 1, 2, 3, 6, 7, 8, 9)


def _a_head_perm():
    idx = []
    for b in range(A_Q_HEADS // 2):
        idx += list(range(b * HEAD_DIM, (b + 1) * HEAD_DIM))
        idx += list(range((A_Q_HEADS // 2 + b) * HEAD_DIM, (A_Q_HEADS // 2 + b + 1) * HEAD_DIM))
    return np.array(idx, dtype=np.int32)


def _l0_proj_kernel(x_ref, g_ref, w_ref, hg_ref, c_ref, s_ref, aq_ref, ak_ref, av_ref, b_ref):
    tm = x_ref.shape[0]
    h = _rms(x_ref[...], g_ref[...]).astype(BF16)
    proj = jnp.dot(h, w_ref[...], preferred_element_type=F32)
    c = c_ref[...]
    s = s_ref[...]
    lane = lax.broadcasted_iota(jnp.int32, (tm, LANES), 1)
    head_lo = lane < HEAD_DIM
    half_lo = (lane & (HEAD_DIM - 1)) < HEAD_DIM // 2

    def norm_rope(blk, gain):
        sq = blk * blk
        s_lo = jnp.sum(jnp.where(head_lo, sq, 0.0), axis=-1, keepdims=True)
        s_hi = jnp.sum(jnp.where(head_lo, 0.0, sq), axis=-1, keepdims=True)
        ms = jnp.where(head_lo, s_lo, s_hi) * (1.0 / HEAD_DIM)
        y = blk * lax.rsqrt(ms + EPS) * gain
        swapped = jnp.where(half_lo, pltpu.roll(y, LANES - HEAD_DIM // 2, 1), pltpu.roll(y, HEAD_DIM // 2, 1))
        return y * c + swapped * s

    outs = []
    for blk in range(L0_IN // LANES):
        v = proj[:, blk * LANES:(blk + 1) * LANES]
        if blk in _L0_NORMED_BLOCKS:
            v = norm_rope(v, hg_ref[:, blk * LANES:(blk + 1) * LANES])
        if blk in _L0_Q_BLOCKS:
            v = v * (HEAD_DIM ** -0.5)
        outs.append(v)
    for b in range(4):
        aq_ref[:, b * LANES:(b + 1) * LANES] = outs[b].astype(BF16)
    ak_ref[...] = outs[4].astype(BF16)
    av_ref[...] = outs[5].astype(BF16)
    for b in range(12):
        b_ref[:, b * LANES:(b + 1) * LANES] = outs[6 + b]


def _l0_proj(x2d, g, w_bf, hg, c0, s0):
    t = x2d.shape[0]
    tm = TM_PROJ
    row = lambda w: pl.BlockSpec((tm, w), lambda i: (i, 0))
    const = lambda a: pl.BlockSpec(a.shape, lambda i: (0,) * a.ndim)
    return pl.pallas_call(
        _l0_proj_kernel,
        grid=(t // tm,),
        in_specs=[row(D_MODEL), const(g), const(w_bf), const(hg), row(LANES), row(LANES)],
        out_specs=[row(A_Q_W), row(LANES), row(LANES), row(3 * B_W)],
        out_shape=[
            jax.ShapeDtypeStruct((t, A_Q_W), BF16),
            jax.ShapeDtypeStruct((t, LANES), BF16),
            jax.ShapeDtypeStruct((t, LANES), BF16),
            jax.ShapeDtypeStruct((t, 3 * B_W), F32),
        ],
        compiler_params=_cparams(("arbitrary",)),
        name="l0_proj",
    )(x2d, g, w_bf, hg, c0, s0)


def _bias_tables(max_dist):
    qi = np.arange(BLOCK)[:, None]
    kj = np.arange(2 * BLOCK)[None, :]
    d_first = qi - kj
    d_later = BLOCK + qi - kj
    out = np.stack([
        np.where((d_first >= 0) & (d_first <= max_dist), 0.0, NEG),
        np.where((d_later >= 0) & (d_later <= max_dist), 0.0, NEG),
    ]).astype(np.float32)
    return out


def _attn_a_kernel(q_ref, kp_ref, kc_ref, vp_ref, vc_ref, bias_ref, sink_ref, o_ref):
    n = pl.program_id(1)
    nh = A_Q_HEADS
    q = q_ref[0]
    lane = lax.broadcasted_iota(jnp.int32, (BLOCK, LANES), 1)
    lo = lane < HEAD_DIM
    zero = jnp.zeros((BLOCK, LANES), BF16)
    blocks = [q[:, b * LANES:(b + 1) * LANES] for b in range(nh // 2)]
    qs = jnp.concatenate([jnp.where(lo, x, zero) for x in blocks] + [jnp.where(lo, zero, x) for x in blocks], axis=0)
    k = jnp.concatenate([kp_ref[0], kc_ref[0]], axis=0)
    v = jnp.concatenate([vp_ref[0], vc_ref[0]], axis=0)
    s = _nt_dot(qs, k)
    bias = bias_ref[jnp.minimum(n, 1)]
    s = (s.reshape(nh, BLOCK, 2 * BLOCK) + bias[None]).reshape(nh * BLOCK, 2 * BLOCK)
    sink = sink_ref[...]
    m = jnp.maximum(jnp.max(s, axis=-1, keepdims=True), sink)
    p = jnp.exp(s - m)
    denom = jnp.sum(p, axis=-1, keepdims=True) + jnp.exp(sink - m)
    o = jnp.dot(p.astype(BF16), v, preferred_element_type=F32) / denom
    half = (nh // 2) * BLOCK
    for b in range(nh // 2):
        o_ref[0, :, b * LANES:(b + 1) * LANES] = jnp.where(
            lo, o[b * BLOCK:(b + 1) * BLOCK], o[half + b * BLOCK:half + (b + 1) * BLOCK]).astype(BF16)


def _attn_a(aq, ak, av, bias, sink_col):
    bsz = aq.shape[0]
    nblk = SEQ // BLOCK
    prev = lambda b, n: (b, jnp.maximum(n - 1, 0), 0)
    cur = lambda b, n: (b, n, 0)
    kv = lambda im: pl.BlockSpec((1, BLOCK, LANES), im)
    return pl.pallas_call(
        _attn_a_kernel,
        grid=(bsz, nblk),
        in_specs=[
            pl.BlockSpec((1, BLOCK, A_Q_W), cur),
            kv(prev), kv(cur), kv(prev), kv(cur),
            pl.BlockSpec(bias.shape, lambda b, n: (0, 0, 0)),
            pl.BlockSpec(sink_col.shape, lambda b, n: (0, 0)),
        ],
        out_specs=pl.BlockSpec((1, BLOCK, A_Q_W), cur),
        out_shape=jax.ShapeDtypeStruct((bsz, SEQ, A_Q_W), BF16),
        compiler_params=_cparams(("arbitrary", "arbitrary")),
        name="l0_attn_swa",
    )(aq, ak, ak, av, av, bias, sink_col)


def _attn_b_kernel(q_ref, k_ref, v_ref, bias_ref, bias16_ref, o_ref, os_ref, ls_ref):
    lane = lax.broadcasted_iota(jnp.int32, (BLOCK, LANES), 1)
    lo = lane < HEAD_DIM
    zero = jnp.zeros((BLOCK, LANES), BF16)

    def block_attn(pi, qstart, kstart, nk, dil, bias):
        qb = q_ref[0, pl.ds(qstart, BLOCK, stride=dil), :].astype(BF16)
        kf = k_ref[0, pl.ds(kstart, nk, stride=dil), :].astype(BF16)
        vf = v_ref[0, pl.ds(kstart, nk, stride=dil), :].astype(BF16)
        qs = jnp.concatenate([jnp.where(lo, qb, zero), jnp.where(lo, zero, qb)], axis=0)
        s = _nt_dot(qs, kf)
        s = (s.reshape(2, BLOCK, nk) + bias[None]).reshape(2 * BLOCK, nk)
        m = jnp.max(s, axis=-1, keepdims=True)
        p = jnp.exp(s - m)
        l = jnp.sum(p, axis=-1, keepdims=True)
        o = jnp.dot(p.astype(BF16), vf, preferred_element_type=F32) / l
        lse = m + jnp.log(l)
        os_ref[pi, pl.ds(qstart, BLOCK, stride=dil), :] = jnp.where(lo, o[:BLOCK], o[BLOCK:])
        ls_ref[pi, pl.ds(qstart, BLOCK, stride=dil), :] = jnp.where(lo, lse[:BLOCK], lse[BLOCK:])

    def body0(n, carry):
        block_attn(0, n * BLOCK, jnp.maximum(n - 1, 0) * BLOCK, 2 * BLOCK, 1, bias_ref[jnp.minimum(n, 1)])
        return carry

    lax.fori_loop(0, SEQ // BLOCK, body0, 0)

    def body1(idx, carry):
        r = lax.shift_right_logical(idx, 2)
        n = idx & 3
        block_attn(1, r + n * (4 * BLOCK), r + jnp.maximum(n - 1, 0) * (4 * BLOCK), 2 * BLOCK, 4,
                   bias_ref[jnp.minimum(n, 1)])
        return carry

    lax.fori_loop(0, 16, body1, 0)

    def body2(r, carry):
        block_attn(2, r, r, BLOCK, 16, bias16_ref[...])
        return carry

    lax.fori_loop(0, 16, body2, 0)

    rows = 256

    def mix(i, carry):
        sl = pl.ds(pl.multiple_of(i * rows, rows), rows)
        l0, l1, l2 = ls_ref[0, sl, :], ls_ref[1, sl, :], ls_ref[2, sl, :]
        mx = jnp.maximum(jnp.maximum(l0, l1), l2)
        e0, e1, e2 = jnp.exp(l0 - mx), jnp.exp(l1 - mx), jnp.exp(l2 - mx)
        tot = e0 + e1 + e2
        out = (e0 / tot) * os_ref[0, sl, :] + (e1 / tot) * os_ref[1, sl, :] + (e2 / tot) * os_ref[2, sl, :]
        o_ref[0, sl, :] = out.astype(BF16)
        return carry

    lax.fori_loop(0, SEQ // rows, mix, 0)


def _attn_b(bqkv, bias, bias16):
    bsz = bqkv.shape[0]
    npair = B_HEADS // 2
    blk = lambda off: pl.BlockSpec((1, SEQ, LANES), lambda b, p: (b, 0, off + p))
    return pl.pallas_call(
        _attn_b_kernel,
        grid=(bsz, npair),
        in_specs=[
            blk(0), blk(npair), blk(2 * npair),
            pl.BlockSpec(bias.shape, lambda b, p: (0, 0, 0)),
            pl.BlockSpec(bias16.shape, lambda b, p: (0, 0)),
        ],
        out_specs=pl.BlockSpec((1, SEQ, LANES), lambda b, p: (b, 0, p)),
        out_shape=jax.ShapeDtypeStruct((bsz, SEQ, B_W), BF16),
        scratch_shapes=[pltpu.VMEM((3, SEQ, LANES), F32), pltpu.VMEM((3, SEQ, LANES), F32)],
        compiler_params=_cparams(("arbitrary", "arbitrary")),
        name="l0_attn_dilated",
    )(bqkv, bqkv, bqkv, bias, bias16)


def _l0_out_ffn_kernel(x_ref, am_ref, bm_ref, wo_ref, g_ref, wg_ref, wu_ref, wd_ref, o_ref, x1_s, h_s, acc_s):
    j = pl.program_id(1)

    @pl.when(j == 0)
    def _():
        x1 = (x_ref[...]
              + jnp.dot(am_ref[...], wo_ref[0:A_Q_W, :], preferred_element_type=F32)
              + jnp.dot(bm_ref[...], wo_ref[A_Q_W:A_Q_W + B_W, :], preferred_element_type=F32))
        x1_s[...] = x1
        h_s[...] = _rms(x1, g_ref[...]).astype(BF16)
        acc_s[...] = jnp.zeros_like(acc_s)

    h = h_s[...]
    gate = jnp.dot(h, wg_ref[...], preferred_element_type=F32)
    up = jnp.dot(h, wu_ref[...], preferred_element_type=F32)
    act = (gate / (1.0 + jnp.exp(-gate))) * up
    acc_s[...] += jnp.dot(act.astype(BF16), wd_ref[...], preferred_element_type=F32)

    @pl.when(j == pl.num_programs(1) - 1)
    def _():
        o_ref[...] = x1_s[...] + acc_s[...]


def _l0_out_ffn(x2d, amix, bmix, wo, g, wg, wu, wd):
    t = x2d.shape[0]
    tm, tf = TM_FFN, TF_FFN
    row = lambda w: pl.BlockSpec((tm, w), lambda i, j: (i, 0))
    const = lambda a: pl.BlockSpec(a.shape, lambda i, j: (0,) * a.ndim)
    return pl.pallas_call(
        _l0_out_ffn_kernel,
        grid=(t // tm, D_FF // tf),
        in_specs=[
            row(D_MODEL), row(A_Q_W), row(B_W), const(wo), const(g),
            pl.BlockSpec((D_MODEL, tf), lambda i, j: (0, j)),
            pl.BlockSpec((D_MODEL, tf), lambda i, j: (0, j)),
            pl.BlockSpec((tf, D_MODEL), lambda i, j: (j, 0)),
        ],
        out_specs=row(D_MODEL),
        out_shape=jax.ShapeDtypeStruct((t, D_MODEL), F32),
        scratch_shapes=[pltpu.VMEM((tm, D_MODEL), F32), pltpu.VMEM((tm, D_MODEL), BF16),
                        pltpu.VMEM((tm, D_MODEL), F32)],
        compiler_params=_cparams(("arbitrary", "arbitrary")),
        name="l0_out_ffn",
    )(x2d, amix, bmix, wo, g, wg, wu, wd)


def _l1_proj_kernel(x_ref, g_ref, win_ref, gq_ref, wuq_ref, gkv_ref, wuk_ref, wuv_ref, gqh_ref, gkh_ref,
                    c_ref, s_ref, q_ref, k_ref, v_ref):
    tm = x_ref.shape[0]
    h = _rms(x_ref[...], g_ref[...]).astype(BF16)
    proj = jnp.dot(h, win_ref[...], preferred_element_type=F32)
    cq = proj[:, :C_Q_RANK]
    ckv = proj[:, C_Q_RANK:C_Q_RANK + C_KV_RANK]
    kpe = proj[:, C_Q_RANK + C_KV_RANK:]
    cqn = _rms(cq, gq_ref[...]).astype(BF16)
    ckvn = _rms(ckv, gkv_ref[...]).astype(BF16)
    q_raw = jnp.dot(cqn, wuq_ref[...], preferred_element_type=F32)
    k_raw = jnp.dot(ckvn, wuk_ref[...], preferred_element_type=F32)
    v_ref[...] = jnp.dot(ckvn, wuv_ref[...], preferred_element_type=F32).astype(BF16)

    c = c_ref[...]
    s = s_ref[...]
    lane = lax.broadcasted_iota(jnp.int32, (tm, LANES), 1)
    first_half = lane < C_NOPE_DIM + C_ROPE_DIM // 2

    def norm_rope(blk, gain):
        ms = jnp.sum(blk * blk, axis=-1, keepdims=True) * (1.0 / C_QK_DIM)
        y = blk * lax.rsqrt(ms + EPS) * gain
        swapped = jnp.where(first_half, pltpu.roll(y, LANES - C_ROPE_DIM // 2, 1), pltpu.roll(y, C_ROPE_DIM // 2, 1))
        return y * c + swapped * s

    gqh = gqh_ref[...]
    gkh = gkh_ref[...]
    scale = C_QK_DIM ** -0.5
    for hd in range(C_HEADS):
        sl = slice(hd * LANES, (hd + 1) * LANES)
        q_ref[:, sl] = (norm_rope(q_raw[:, sl], gqh) * scale).astype(BF16)
        k_ref[:, sl] = norm_rope(k_raw[:, sl] + kpe, gkh).astype(BF16)


def _l1_proj(x2d, g, win, gq, wuq, gkv, wuk, wuv, gqh, gkh, c1, s1):
    t = x2d.shape[0]
    tm = TM_PROJ
    row = lambda w: pl.BlockSpec((tm, w), lambda i: (i, 0))
    const = lambda a: pl.BlockSpec(a.shape, lambda i: (0,) * a.ndim)
    return pl.pallas_call(
        _l1_proj_kernel,
        grid=(t // tm,),
        in_specs=[row(D_MODEL), const(g), const(win), const(gq), const(wuq), const(gkv), const(wuk), const(wuv),
                  const(gqh), const(gkh), row(LANES), row(LANES)],
        out_specs=[row(C_HEADS * LANES), row(C_HEADS * LANES), row(C_HEADS * C_V_DIM)],
        out_shape=[
            jax.ShapeDtypeStruct((t, C_HEADS * LANES), BF16),
            jax.ShapeDtypeStruct((t, C_HEADS * LANES), BF16),
            jax.ShapeDtypeStruct((t, C_HEADS * C_V_DIM), BF16),
        ],
        compiler_params=_cparams(("arbitrary",)),
        name="l1_proj",
    )(x2d, g, win, gq, wuq, gkv, wuk, wuv, gqh, gkh, c1, s1)


def _l1_attn_kernel(q_ref, k_ref, v_ref, bias_ref, o_ref, m_s, l_s, acc_s):
    n = pl.program_id(2)
    tq = q_ref.shape[1]
    tk = tq
    lane = lax.broadcasted_iota(jnp.int32, (tq, LANES), 1)
    lo = lane < C_V_DIM
    outs = []
    for hh in range(2):
        q = q_ref[0, :, hh * LANES:(hh + 1) * LANES]
        m_s[...] = jnp.full(m_s.shape, NEG, F32)
        l_s[...] = jnp.zeros_like(l_s)
        acc_s[...] = jnp.zeros_like(acc_s)

        def step(t, masked):
            ks = pl.multiple_of(t * tk, tk)
            kt = k_ref[0, pl.ds(ks, tk), hh * LANES:(hh + 1) * LANES]
            vt = v_ref[0, pl.ds(ks, tk), :]
            s = _nt_dot(q, kt)
            if masked:
                s = s + bias_ref[...]
            m_prev = m_s[...]
            m_next = jnp.maximum(m_prev, jnp.max(s, axis=-1, keepdims=True))
            alpha = jnp.exp(m_prev - m_next)
            p = jnp.exp(s - jnp.concatenate([m_next] * (tk // LANES), axis=1))
            l_s[...] = alpha * l_s[...] + jnp.sum(p, axis=-1, keepdims=True)
            acc_s[...] = alpha * acc_s[...] + jnp.dot(p.astype(BF16), vt, preferred_element_type=F32)
            m_s[...] = m_next

        def body(t, carry):
            step(t, False)
            return carry

        lax.fori_loop(0, n, body, 0)
        step(n, True)
        outs.append(acc_s[...] / l_s[...])
    o_ref[0] = jnp.where(lo, outs[0], outs[1]).astype(BF16)


def _l1_attn(q, k, v, bias):
    bsz = q.shape[0]
    tq = TQ_MLA
    npair = C_HEADS // 2
    return pl.pallas_call(
        _l1_attn_kernel,
        grid=(bsz, npair, SEQ // tq),
        in_specs=[
            pl.BlockSpec((1, tq, 2 * LANES), lambda b, p, n: (b, n, p)),
            pl.BlockSpec((1, SEQ, 2 * LANES), lambda b, p, n: (b, 0, p)),
            pl.BlockSpec((1, SEQ, LANES), lambda b, p, n: (b, 0, p)),
            pl.BlockSpec(bias.shape, lambda b, p, n: (0, 0)),
        ],
        out_specs=pl.BlockSpec((1, tq, LANES), lambda b, p, n: (b, n, p)),
        out_shape=jax.ShapeDtypeStruct((bsz, SEQ, C_HEADS * C_V_DIM), BF16),
        scratch_shapes=[pltpu.VMEM((tq, LANES), F32), pltpu.VMEM((tq, LANES), F32), pltpu.VMEM((tq, LANES), F32)],
        compiler_params=_cparams(("arbitrary", "arbitrary", "arbitrary")),
        name="l1_attn_mla",
    )(q, k, v, bias)


_META_E1, _META_E2, _META_G1, _META_G2, _META_R1, _META_R2 = range(6)


def _l1_out_route_kernel(x_ref, a_ref, wo_ref, g_ref, rhi_ref, rlo_ref, tri_ref,
                         x3_ref, h_ref, meta_ref, cnt_ref, carry_s):
    i = pl.program_id(0)
    tm = x_ref.shape[0]

    @pl.when(i == 0)
    def _():
        carry_s[...] = jnp.zeros_like(carry_s)

    x3 = x_ref[...] + jnp.dot(a_ref[...], wo_ref[...], preferred_element_type=F32)
    x3_ref[...] = x3
    h = _rms(x3, g_ref[...])
    h_ref[...] = h
    h_hi = h.astype(BF16)
    h_lo = (h - h_hi.astype(F32)).astype(BF16)
    rhi = rhi_ref[...]
    logits = (jnp.dot(h_hi, rhi, preferred_element_type=F32)
              + (jnp.dot(h_lo, rhi, preferred_element_type=F32)
                 + jnp.dot(h_hi, rlo_ref[...], preferred_element_type=F32)))
    lane_i = lax.broadcasted_iota(jnp.int32, (tm, LANES), 1)
    lane = lane_i.astype(F32)
    logits = jnp.where(lane_i < N_EXPERTS, logits, NEG)
    m1 = jnp.max(logits, axis=-1, keepdims=True)
    i1 = jnp.min(jnp.where(logits == m1, lane, float(LANES)), axis=-1, keepdims=True)
    rest = jnp.where(lane == i1, NEG, logits)
    m2 = jnp.max(rest, axis=-1, keepdims=True)
    i2 = jnp.min(jnp.where(rest == m2, lane, float(LANES)), axis=-1, keepdims=True)
    e = jnp.exp(m2 - m1)
    g1 = 1.0 / (1.0 + e)
    g2 = e / (1.0 + e)
    oh1 = (lane == i1).astype(F32)
    oh2 = (lane == i2).astype(F32)
    cnt = oh1 + oh2
    before = jnp.dot(tri_ref[...], cnt.astype(BF16), preferred_element_type=F32) + carry_s[...]
    r1 = jnp.sum(oh1 * before, axis=-1, keepdims=True)
    r2 = jnp.sum(oh2 * before, axis=-1, keepdims=True)
    carry_s[...] = carry_s[...] + jnp.sum(cnt, axis=0, keepdims=True)
    cnt_ref[...] = carry_s[...]
    meta = jnp.zeros((tm, LANES), F32)
    for col, val in ((_META_E1, i1), (_META_E2, i2), (_META_G1, g1), (_META_G2, g2), (_META_R1, r1), (_META_R2, r2)):
        meta = jnp.where(lane_i == col, val, meta)
    meta_ref[...] = meta


def _l1_out_route(x2d, attn, wo, g, rhi, rlo, tri):
    t = x2d.shape[0]
    tm = TM_ROUTE
    row = lambda w: pl.BlockSpec((tm, w), lambda i: (i, 0))
    const = lambda a: pl.BlockSpec(a.shape, lambda i: (0,) * a.ndim)
    return pl.pallas_call(
        _l1_out_route_kernel,
        grid=(t // tm,),
        in_specs=[row(D_MODEL), row(D_MODEL), const(wo), const(g), const(rhi), const(rlo), const(tri)],
        out_specs=[row(D_MODEL), row(D_MODEL), row(LANES), pl.BlockSpec((1, LANES), lambda i: (0, 0))],
        out_shape=[
            jax.ShapeDtypeStruct((t, D_MODEL), F32),
            jax.ShapeDtypeStruct((t, D_MODEL), F32),
            jax.ShapeDtypeStruct((t, LANES), F32),
            jax.ShapeDtypeStruct((1, LANES), F32),
        ],
        scratch_shapes=[pltpu.VMEM((1, LANES), F32)],
        compiler_params=_cparams(("arbitrary",)),
        name="l1_out_route",
    )(x2d, attn, wo, g, rhi, rlo, tri)


def _row_copy(src_hbm, row, dst, slot, sem):
    return pltpu.make_async_copy(src_hbm.at[pl.ds(row, 1), :], dst.at[pl.ds(slot, 1), :], sem)


def _expert_kernel(te_ref, nu_ref, tok_ref, h_hbm, wg_ref, wu_ref, wd_ref, o_ref, xs_s, xb_s, acc_s, sem):
    i = pl.program_id(0)
    j = pl.program_id(1)
    tm = xs_s.shape[0]
    used = i < nu_ref[0]

    @pl.when(jnp.logical_and(used, j == 0))
    def _():
        def issue(r, carry):
            _row_copy(h_hbm, tok_ref[0, 0, r], xs_s, r, sem).start()
            return carry

        lax.fori_loop(0, tm, issue, 0)

        def wait(r, carry):
            _row_copy(h_hbm, 0, xs_s, r, sem).wait()
            return carry

        lax.fori_loop(0, tm, wait, 0)
        xb_s[...] = xs_s[...].astype(BF16)
        acc_s[...] = jnp.zeros_like(acc_s)

    @pl.when(used)
    def _():
        xb = xb_s[...]
        gate = jnp.dot(xb, wg_ref[0], preferred_element_type=F32)
        up = jnp.dot(xb, wu_ref[0], preferred_element_type=F32)
        act = (gate / (1.0 + jnp.exp(-gate))) * up
        acc_s[...] += jnp.dot(act.astype(BF16), wd_ref[0], preferred_element_type=F32)

    last = j == pl.num_programs(1) - 1

    @pl.when(jnp.logical_and(used, last))
    def _():
        o_ref[...] = acc_s[...]

    @pl.when(jnp.logical_and(jnp.logical_not(used), last))
    def _():
        o_ref[...] = jnp.zeros_like(o_ref)


def _experts(tile_expert, n_used, row_tok, h, wg, wu, wd):
    nt = tile_expert.shape[0]
    tm, tf = TM_EXPERT, TF_EXPERT
    grid_spec = pltpu.PrefetchScalarGridSpec(
        num_scalar_prefetch=2,
        grid=(nt, D_FF_EXPERT // tf),
        in_specs=[
            pl.BlockSpec((1, 1, tm), lambda i, j, te, nu: (i, 0, 0), memory_space=pltpu.SMEM),
            pl.BlockSpec(memory_space=pl.ANY),
            pl.BlockSpec((1, D_MODEL, tf), lambda i, j, te, nu: (te[i], 0, j)),
            pl.BlockSpec((1, D_MODEL, tf), lambda i, j, te, nu: (te[i], 0, j)),
            pl.BlockSpec((1, tf, D_MODEL), lambda i, j, te, nu: (te[i], j, 0)),
        ],
        out_specs=pl.BlockSpec((tm, D_MODEL), lambda i, j, te, nu: (i, 0)),
        scratch_shapes=[pltpu.VMEM((tm, D_MODEL), F32), pltpu.VMEM((tm, D_MODEL), BF16),
                        pltpu.VMEM((tm, D_MODEL), F32), pltpu.SemaphoreType.DMA(())],
    )
    return pl.pallas_call(
        _expert_kernel,
        grid_spec=grid_spec,
        out_shape=jax.ShapeDtypeStruct((nt * tm, D_MODEL), F32),
        compiler_params=_cparams(("arbitrary", "arbitrary")),
        name="l1_experts",
    )(tile_expert, n_used, row_tok.reshape(nt, 1, tm), h, wg, wu, wd)


def _combine_kernel(dest_ref, x_ref, meta_ref, ys_hbm, o_ref, buf_s, sem):
    tm = x_ref.shape[0]

    def issue(r, carry):
        _row_copy(ys_hbm, dest_ref[0, 0, r], buf_s.at[0], r, sem).start()
        _row_copy(ys_hbm, dest_ref[0, 0, tm + r], buf_s.at[1], r, sem).start()
        return carry

    lax.fori_loop(0, tm, issue, 0)

    def wait(r, carry):
        _row_copy(ys_hbm, 0, buf_s.at[0], r, sem).wait()
        _row_copy(ys_hbm, 0, buf_s.at[1], r, sem).wait()
        return carry

    lax.fori_loop(0, tm, wait, 0)
    meta = meta_ref[...]
    g1 = meta[:, _META_G1:_META_G1 + 1]
    g2 = meta[:, _META_G2:_META_G2 + 1]
    o_ref[...] = x_ref[...] + (g1 * buf_s[0] + g2 * buf_s[1])


def _combine(dest, x3, meta, ys):
    t = x3.shape[0]
    tm = TM_COMBINE
    row = lambda w: pl.BlockSpec((tm, w), lambda i: (i, 0))
    return pl.pallas_call(
        _combine_kernel,
        grid=(t // tm,),
        in_specs=[
            pl.BlockSpec((1, 1, 2 * tm), lambda i: (i, 0, 0), memory_space=pltpu.SMEM),
            row(D_MODEL), row(LANES),
            pl.BlockSpec(memory_space=pl.ANY),
        ],
        out_specs=row(D_MODEL),
        out_shape=jax.ShapeDtypeStruct((t, D_MODEL), F32),
        scratch_shapes=[pltpu.VMEM((2, tm, D_MODEL), F32), pltpu.SemaphoreType.DMA(())],
        compiler_params=_cparams(("arbitrary",)),
        name="l1_combine",
    )(dest, x3, meta, ys)


def _layer0(x2d, bsz, cos, sin, attn_norm, w_in, a_qn, a_kn, a_sinks, b_qn, b_kn, w_out, ffn_norm,
            w_gate, w_up, w_down):
    t = x2d.shape[0]
    perm = _a_head_perm()
    col_perm = np.concatenate([perm, np.arange(A_Q_W, L0_IN, dtype=np.int32)])
    w_in_p = w_in[:, col_perm].astype(BF16)
    ones = jnp.ones((LANES,), F32)
    hg = jnp.concatenate([
        jnp.tile(a_qn, A_Q_HEADS), jnp.tile(a_kn, A_KV_HEADS), ones,
        jnp.tile(b_qn, B_HEADS), jnp.tile(b_kn, B_HEADS), jnp.ones((B_W,), F32),
    ]).reshape(1, L0_IN)
    c0 = jnp.tile(cos, (1, 4))
    s0 = jnp.concatenate([-sin, sin, -sin, sin], axis=-1)
    aq, ak, av, bqkv = _l0_proj(x2d, attn_norm.reshape(1, D_MODEL), w_in_p, hg, c0, s0)

    r3 = lambda a: a.reshape(bsz, SEQ, a.shape[-1])
    sink_col = jnp.repeat(a_sinks.astype(F32), BLOCK).reshape(A_Q_HEADS * BLOCK, 1)
    amix = _attn_a(r3(aq), r3(ak), r3(av), jnp.asarray(_bias_tables(A_WINDOW - 1)), sink_col)
    bias_b = _bias_tables(BLOCK)
    bmix = _attn_b(r3(bqkv), jnp.asarray(bias_b), jnp.asarray(bias_b[0, :, :BLOCK]))

    mix_perm = np.concatenate([perm, np.arange(A_Q_W, A_Q_W + B_W, dtype=np.int32)])
    wo_p = w_out[mix_perm, :].astype(BF16)
    return _l0_out_ffn(x2d, amix.reshape(t, A_Q_W), bmix.reshape(t, B_W), wo_p, ffn_norm.reshape(1, D_MODEL),
                       w_gate.astype(BF16), w_up.astype(BF16), w_down.astype(BF16))


def _pad_heads(w, per_head, keep):
    k = w.shape[0]
    w = w.reshape(k, C_HEADS, per_head)[:, :, :keep]
    return jnp.pad(w, ((0, 0), (0, 0), (0, LANES - keep))).reshape(k, C_HEADS * LANES)


def _layer1(x2d, bsz, cos, sin, attn_norm, w_in, q_a_norm, w_uq, kv_a_norm, w_ukv, c_qn, c_kn, w_out, ffn_norm,
            router, we_gate, we_up, we_down):
    t = x2d.shape[0]
    lat = C_Q_RANK + C_KV_RANK
    w_in_p = jnp.concatenate([
        w_in[:, :lat], jnp.zeros((D_MODEL, C_NOPE_DIM), F32), w_in[:, lat:],
        jnp.zeros((D_MODEL, LANES - C_QK_DIM), F32)], axis=1).astype(BF16)
    wuq_p = _pad_heads(w_uq, C_QK_DIM, C_QK_DIM).astype(BF16)
    wuk_p = _pad_heads(w_ukv, C_NOPE_DIM + C_V_DIM, C_NOPE_DIM).astype(BF16)
    wuv = w_ukv.reshape(C_KV_RANK, C_HEADS, C_NOPE_DIM + C_V_DIM)[:, :, C_NOPE_DIM:].reshape(
        C_KV_RANK, C_HEADS * C_V_DIM).astype(BF16)
    pad = LANES - C_QK_DIM
    gqh = jnp.pad(c_qn, (0, pad)).reshape(1, LANES)
    gkh = jnp.pad(c_kn, (0, pad)).reshape(1, LANES)
    c1 = jnp.concatenate([jnp.ones((t, C_NOPE_DIM), F32), cos, cos, jnp.ones((t, pad), F32)], axis=-1)
    s1 = jnp.concatenate([jnp.zeros((t, C_NOPE_DIM), F32), -sin, sin, jnp.zeros((t, pad), F32)], axis=-1)
    q, k, v = _l1_proj(x2d, attn_norm.reshape(1, D_MODEL), w_in_p, q_a_norm.reshape(1, C_Q_RANK), wuq_p,
                       kv_a_norm.reshape(1, C_KV_RANK), wuk_p, wuv, gqh, gkh, c1, s1)

    tq = TQ_MLA
    causal = np.where(np.arange(tq)[:, None] >= np.arange(tq)[None, :], 0.0, NEG).astype(np.float32)
    r3 = lambda a: a.reshape(bsz, SEQ, a.shape[-1])
    attn = _l1_attn(r3(q), r3(k), r3(v), jnp.asarray(causal)).reshape(t, C_HEADS * C_V_DIM)

    r_pad = jnp.pad(router, ((0, 0), (0, LANES - N_EXPERTS)))
    r_hi = r_pad.astype(BF16)
    r_lo = (r_pad - r_hi.astype(F32)).astype(BF16)
    tri = jnp.asarray(np.tril(np.ones((TM_ROUTE, TM_ROUTE), np.float32), -1), dtype=BF16)
    x3, h, meta, counts = _l1_out_route(x2d, attn, w_out.astype(BF16), ffn_norm.reshape(1, D_MODEL), r_hi, r_lo, tri)

    tm = TM_EXPERT
    nt = (t * TOP_K) // tm + N_EXPERTS
    counts = counts[0, :N_EXPERTS].astype(jnp.int32)
    padded = (counts + tm - 1) // tm * tm
    pend = jnp.cumsum(padded)
    pstart = pend - padded
    e_idx = meta[:, _META_E1:_META_E2 + 1].astype(jnp.int32)
    rank = meta[:, _META_R1:_META_R2 + 1].astype(jnp.int32)
    dest = pstart[e_idx] + rank
    tok = jnp.broadcast_to(jnp.arange(t, dtype=jnp.int32)[:, None], (t, TOP_K))
    row_tok = jnp.zeros((nt * tm,), jnp.int32).at[dest.reshape(-1)].set(tok.reshape(-1))
    n_used = (pend[-1] // tm).astype(jnp.int32).reshape(1)
    tile_e = jnp.searchsorted(pend, jnp.arange(nt, dtype=jnp.int32) * tm, side='right').astype(jnp.int32)
    last_e = jnp.max(jnp.where(counts > 0, jnp.arange(N_EXPERTS, dtype=jnp.int32), 0))
    tile_e = jnp.minimum(tile_e, last_e)

    ys = _experts(tile_e, n_used, row_tok, h, we_gate.astype(BF16), we_up.astype(BF16), we_down.astype(BF16))
    tmc = TM_COMBINE
    dest_blk = dest.reshape(t // tmc, tmc, TOP_K).transpose(0, 2, 1).reshape(t // tmc, 1, TOP_K * tmc)
    return _combine(dest_blk, x3, meta, ys)


def kernel(x, positions, l0_attn_norm, l0_w_in, l0_a_q_norm, l0_a_k_norm, l0_a_sinks, l0_b_q_norm, l0_b_k_norm,
           l0_w_out, l0_ffn_norm, l0_w_gate, l0_w_up, l0_w_down, l1_attn_norm, l1_w_in, l1_q_a_norm, l1_w_uq,
           l1_kv_a_norm, l1_w_ukv, l1_c_q_norm, l1_c_k_norm, l1_w_out, l1_ffn_norm, l1_router, l1_we_gate,
           l1_we_up, l1_we_down):
    bsz, seq, d = x.shape
    assert seq == SEQ and d == D_MODEL
    x2d = x.reshape(bsz * seq, d)
    cos_h, sin_h = _rope_tables(positions, HEAD_DIM)
    cos_c, sin_c = _rope_tables(positions, C_ROPE_DIM)
    x2d = _layer0(x2d, bsz, cos_h, sin_h, l0_attn_norm, l0_w_in, l0_a_q_norm, l0_a_k_norm, l0_a_sinks,
                  l0_b_q_norm, l0_b_k_norm, l0_w_out, l0_ffn_norm, l0_w_gate, l0_w_up, l0_w_down)
    x2d = _layer1(x2d, bsz, cos_c, sin_c, l1_attn_norm, l1_w_in, l1_q_a_norm, l1_w_uq, l1_kv_a_norm, l1_w_ukv,
                  l1_c_q_norm, l1_c_k_norm, l1_w_out, l1_ffn_norm, l1_router, l1_we_gate, l1_we_up, l1_we_down)
    return x2d.reshape(bsz, seq, d)
```
